```python
import jax, jax.numpy as jnp
from jax import lax
import numpy as np

D_MODEL = 4096
BATCH = 4
SEQ = 4096
DEPTH = 1
DEC_BATCH = 8
DEC_SEQ = 2048
PAST_LEN = 128

D_CONV = D_MODEL // 2
CONV_WIDTH = 31
D_POOL = D_MODEL // 2
POOL_WINDOWS = (2, 4, 8, 16)
N_POOL_GROUPS = len(POOL_WINDOWS)
POOL_GROUP_DIM = D_POOL // N_POOL_GROUPS
N_BRANCHES = 2
D_IN = 2 * D_CONV + D_POOL + N_BRANCHES * D_MODEL
N_GROUPS = 4
EXPERTS_PER_GROUP = 8
N_EXPERTS = N_GROUPS * EXPERTS_PER_GROUP
TOP_K = 2
D_EXPERT = D_MODEL // 4
ROUTE_BLOCK = 256
D_PLE = 256
EPS = 1e-6

kernel_name = 'hybrid_conv_pool_hmoe_encoder'


def rmsnorm(x, g):
    xf = x.astype(jnp.float32)
    y = xf * lax.rsqrt(jnp.mean(xf * xf, axis=-1, keepdims=True) + EPS)
    return (y * g.astype(jnp.float32)).astype(x.dtype)


def layernorm(x, g, b):
    xf = x.astype(jnp.float32)
    mu = jnp.mean(xf, axis=-1, keepdims=True)
    xc = xf - mu
    y = xc * lax.rsqrt(jnp.mean(xc * xc, axis=-1, keepdims=True) + EPS)
    return (y * g.astype(jnp.float32) + b.astype(jnp.float32)).astype(x.dtype)


def depthwise_conv(v, w, b):
    pad = CONV_WIDTH // 2
    out = lax.conv_general_dilated(
        v, w.astype(v.dtype)[:, None, :], window_strides=(1,), padding=[(pad, pad)],
        dimension_numbers=('NWC', 'WIO', 'NWC'), feature_group_count=v.shape[-1])
    return out + b.astype(v.dtype)


def conv_mixer(a_val, a_gate, w_dw, b_dw, ln_g, ln_b):
    a = a_val * jax.nn.sigmoid(a_gate)
    a = depthwise_conv(a, w_dw, b_dw)
    return jax.nn.silu(layernorm(a, ln_g, ln_b))


def centred_mean_minus_self(u, window):
    bsz, seq, ch = u.shape
    uf = u.astype(jnp.float32)
    cs = jnp.concatenate([jnp.zeros((bsz, 1, ch), jnp.float32), jnp.cumsum(uf, axis=1)], axis=1)
    left = window // 2
    right = window - left - 1
    t = jnp.arange(seq)
    hi = jnp.minimum(t + right + 1, seq)
    lo = jnp.maximum(t - left, 0)
    total = jnp.take(cs, hi, axis=1) - jnp.take(cs, lo, axis=1)
    count = (hi - lo).astype(jnp.float32)
    return (total / count[None, :, None] - uf).astype(u.dtype)


def pool_mixer(u, w_mix, scale):
    groups = [centred_mean_minus_self(u[..., gi * POOL_GROUP_DIM:(gi + 1) * POOL_GROUP_DIM], POOL_WINDOWS[gi])
              for gi in range(N_POOL_GROUPS)]
    z = jnp.stack(groups, axis=2)
    z = jnp.einsum('bsgc,gcd->bsgd', z, w_mix.astype(z.dtype))
    return z.reshape(u.shape) * scale.astype(u.dtype)


def hierarchical_moe(h, w_group_router, w_expert_router, w_gate_up, w_down):
    n_tok, d = h.shape
    glog = jnp.dot(h, w_group_router).astype(jnp.float32)
    gprob = jax.nn.softmax(glog, axis=-1)
    gsel = jnp.argmax(glog, axis=-1).astype(jnp.int32)
    gweight = jnp.take_along_axis(gprob, gsel[:, None], axis=-1)
    elog = jnp.dot(h, w_expert_router).astype(jnp.float32).reshape(n_tok, N_GROUPS, EXPERTS_PER_GROUP)
    elog = jnp.take_along_axis(elog, gsel[:, None, None], axis=1)[:, 0]
    eprob = jax.nn.softmax(elog, axis=-1)
    topv, topi = lax.top_k(eprob, TOP_K)
    comb = gweight * topv / jnp.sum(topv, axis=-1, keepdims=True)
    eid = (gsel[:, None] * EXPERTS_PER_GROUP + topi.astype(jnp.int32)).reshape(-1)
    wts = comb.reshape(-1)
    tok = jnp.repeat(jnp.arange(n_tok, dtype=jnp.int32), TOP_K)
    n_assign = n_tok * TOP_K

    order = jnp.argsort(eid, stable=True)
    se, stok, sw = eid[order], tok[order], wts[order]
    counts = jnp.bincount(eid, length=N_EXPERTS).astype(jnp.int32)
    start = jnp.cumsum(counts) - counts
    padded = (counts + ROUTE_BLOCK - 1) // ROUTE_BLOCK * ROUTE_BLOCK
    pend = jnp.cumsum(padded)
    pstart = pend - padded
    dest = pstart[se] + (jnp.arange(n_assign, dtype=jnp.int32) - start[se])
    n_blocks = -(-n_assign // ROUTE_BLOCK) + N_EXPERTS
    n_rows = n_blocks * ROUTE_BLOCK
    row_tok = jnp.zeros((n_rows,), jnp.int32).at[dest].set(stok)
    row_w = jnp.zeros((n_rows,), h.dtype).at[dest].set(sw.astype(h.dtype))
    blk_start = jnp.arange(n_blocks, dtype=jnp.int32) * ROUTE_BLOCK
    blk_expert = jnp.minimum(jnp.searchsorted(pend, blk_start, side='right'), N_EXPERTS - 1).astype(jnp.int32)

    def block_step(out, blk):
        btok, bw, be = blk
        xb = h[btok]
        gu = jnp.dot(xb, w_gate_up[be])
        yb = jnp.dot(jax.nn.silu(gu[:, :D_EXPERT]) * gu[:, D_EXPERT:], w_down[be])
        return out.at[btok].add(yb * bw[:, None]), None

    out, _ = lax.scan(block_step, jnp.zeros((n_tok, d), h.dtype),
                      (row_tok.reshape(n_blocks, ROUTE_BLOCK), row_w.reshape(n_blocks, ROUTE_BLOCK), blk_expert))
    return out


def trunk(x, p, w_in, b_in, w_dw, b_dw, ln_g, ln_b, w_pool_mix, pool_scale, w_conv_out, w_pool_out,
          w_out, g_mix, g_ffn, w_group_router, w_expert_router, w_gate_up, w_down, g_ple, w_ple_proj,
          w_ple_gate, g_final):
    bsz, seq, d = x.shape
    o_pool = 2 * D_CONV
    o_gate = 2 * D_CONV + D_POOL
    for i in range(DEPTH):
        h = rmsnorm(x, g_mix[i])
        proj = jnp.dot(h, w_in[i]) + b_in[i].astype(h.dtype)
        a = conv_mixer(proj[..., :D_CONV], proj[..., D_CONV:o_pool], w_dw[i], b_dw[i], ln_g[i], ln_b[i])
        y_a = jnp.dot(a, w_conv_out[i])
        z = pool_mixer(proj[..., o_pool:o_gate], w_pool_mix[i], pool_scale[i])
        y_b = jnp.dot(z, w_pool_out[i])
        mixed = (jax.nn.sigmoid(proj[..., o_gate:o_gate + d]) * y_a
                 + jax.nn.sigmoid(proj[..., o_gate + d:]) * y_b)
        x = x + jnp.dot(mixed, w_out[i])
        h2 = rmsnorm(x, g_ffn[i]).reshape(bsz * seq, d)
        x = x + hierarchical_moe(h2, w_group_router[i], w_expert_router[i], w_gate_up[i], w_down[i]).reshape(bsz, seq, d)
        e = jnp.dot(p[i].astype(x.dtype), w_ple_proj[i])
        gate = jax.nn.sigmoid(jnp.dot(rmsnorm(x, g_ple[i]), w_ple_gate[i]))
        x = x + e * gate
    return rmsnorm(x, g_final)


def _normal(k, shape, scale):
    return jax.random.normal(k, shape, jnp.float32) * scale


def setup_inputs(seed: int = 0) -> dict:
    key = jax.random.key(seed)
    ks = jax.random.split(key, 26)
    L, D = DEPTH, D_MODEL
    return {
        'x_prompt': _normal(ks[0], (BATCH, SEQ, D), 1.0),
        'x_sample': _normal(ks[1], (DEC_BATCH, DEC_SEQ, D), 1.0),
        'p_prompt': _normal(ks[2], (DEPTH, BATCH, SEQ, D_PLE), 1.0),
        'p_sample': _normal(ks[3], (DEPTH, DEC_BATCH, DEC_SEQ, D_PLE), 1.0),
        'w_in': _normal(ks[4], (L, D, D_IN), D ** -0.5),
        'b_in': _normal(ks[5], (L, D_IN), 0.02),
        'w_dw': _normal(ks[6], (L, CONV_WIDTH, D_CONV), CONV_WIDTH ** -0.5),
        'b_dw': _normal(ks[7], (L, D_CONV), 0.02),
        'ln_g': 1.0 + _normal(ks[8], (L, D_CONV), 0.02),
        'ln_b': _normal(ks[9], (L, D_CONV), 0.02),
        'w_pool_mix': _normal(ks[10], (L, N_POOL_GROUPS, POOL_GROUP_DIM, POOL_GROUP_DIM), POOL_GROUP_DIM ** -0.5),
        'pool_scale': 1.0 + _normal(ks[11], (L, D_POOL), 0.02),
        'w_conv_out': _normal(ks[12], (L, D_CONV, D), D_CONV ** -0.5),
        'w_pool_out': _normal(ks[13], (L, D_POOL, D), D_POOL ** -0.5),
        'w_out': _normal(ks[14], (L, D, D), D ** -0.5),
        'g_mix': 1.0 + _normal(ks[15], (L, D), 0.02),
        'g_ffn': 1.0 + _normal(ks[16], (L, D), 0.02),
        'w_group_router': _normal(ks[17], (L, D, N_GROUPS), D ** -0.5),
        'w_expert_router': _normal(ks[18], (L, D, N_EXPERTS), D ** -0.5),
        'w_gate_up': _normal(ks[19], (L, N_EXPERTS, D, 2 * D_EXPERT), D ** -0.5),
        'w_down': _normal(ks[20], (L, N_EXPERTS, D_EXPERT, D), D_EXPERT ** -0.5),
        'g_ple': 1.0 + _normal(ks[21], (L, D), 0.02),
        'w_ple_proj': _normal(ks[22], (L, D_PLE, D), D_PLE ** -0.5),
        'w_ple_gate': _normal(ks[23], (L, D, D), D ** -0.5),
        'g_final': 1.0 + _normal(ks[24], (D,), 0.02),
    }


def reference(x_prompt, x_sample, p_prompt, p_sample, w_in, b_in, w_dw, b_dw, ln_g, ln_b, w_pool_mix,
              pool_scale, w_conv_out, w_pool_out, w_out, g_mix, g_ffn, w_group_router, w_expert_router,
              w_gate_up, w_down, g_ple, w_ple_proj, w_ple_gate, g_final):
    y_prompt = trunk(x_prompt, p_prompt, w_in, b_in, w_dw, b_dw, ln_g, ln_b, w_pool_mix, pool_scale,
                     w_conv_out, w_pool_out, w_out, g_mix, g_ffn, w_group_router, w_expert_router,
                     w_gate_up, w_down, g_ple, w_ple_proj, w_ple_gate, g_final)
    y_sample = trunk(x_sample, p_sample, w_in, b_in, w_dw, b_dw, ln_g, ln_b, w_pool_mix, pool_scale,
                     w_conv_out, w_pool_out, w_out, g_mix, g_ffn, w_group_router, w_expert_router,
                     w_gate_up, w_down, g_ple, w_ple_proj, w_ple_gate, g_final)
    return (y_prompt, y_sample)
```

```python
import functools

import jax
import jax.numpy as jnp
import numpy as np
from jax import lax
from jax.experimental import pallas as pl
from jax.experimental.pallas import tpu as pltpu

LANE = 128
SUBLANE = 8
MIB = 1024 * 1024

EPS = 1e-6
POOL_HALO = 8
CONV_HALO = 16
TOP_K = 2
ROW_BLOCK = 256
NEG = -1e30


def _params(semantics, vmem_mib):
    return pltpu.CompilerParams(dimension_semantics=semantics,
                                vmem_limit_bytes=vmem_mib * MIB)


def _tile(n, pref, align=SUBLANE):
    t = min(pref, n)
    t -= t % align
    while t > align and n % t:
        t -= align
    assert t > 0 and n % t == 0, (n, pref, align)
    return t


def _prenorm_body(xa_ref, xb_ref, g_ref, o_ref, *, nta):
    i = pl.program_id(0)
    x = jnp.where(i < nta, xa_ref[...], xb_ref[...])
    ms = jnp.mean(x * x, axis=-1, keepdims=True)
    o_ref[...] = (x * lax.rsqrt(ms + EPS) * g_ref[...]).astype(o_ref.dtype)


def _prenorm(xa, xb, g, tm):
    na, d = xa.shape
    nb = xb.shape[0]
    nta, ntb = na // tm, nb // tm
    return pl.pallas_call(
        functools.partial(_prenorm_body, nta=nta),
        grid=(nta + ntb,),
        in_specs=[
            pl.BlockSpec((tm, d), lambda i: (jnp.minimum(i, nta - 1), 0)),
            pl.BlockSpec((tm, d), lambda i: (jnp.maximum(i - nta, 0), 0)),
            pl.BlockSpec((1, d), lambda i: (0, 0)),
        ],
        out_specs=pl.BlockSpec((tm, d), lambda i: (i, 0)),
        out_shape=jax.ShapeDtypeStruct((na + nb, d), jnp.bfloat16),
        compiler_params=_params(("parallel",), 40),
        name="prenorm",
    )(xa, xb, g)


def _glu_body(h_ref, wv_ref, wg_ref, bv_ref, bg_ref, o_ref):
    h = h_ref[...]
    v = jnp.dot(h, wv_ref[...], preferred_element_type=jnp.float32) + bv_ref[...]
    g = jnp.dot(h, wg_ref[...], preferred_element_type=jnp.float32) + bg_ref[...]
    o_ref[...] = (v * jax.nn.sigmoid(g)).astype(o_ref.dtype)


def _linear_body(h_ref, w_ref, b_ref, o_ref, *, sigmoid):
    y = jnp.dot(h_ref[...], w_ref[...], preferred_element_type=jnp.float32) + b_ref[...]
    if sigmoid:
        y = jax.nn.sigmoid(y)
    o_ref[...] = y.astype(o_ref.dtype)


def _inproj_glu(h, w, b, n_out, gate_off, tm, tn):
    n, k = h.shape
    go = gate_off // tn
    return pl.pallas_call(
        _glu_body,
        grid=(n // tm, n_out // tn),
        in_specs=[
            pl.BlockSpec((tm, k), lambda i, j: (i, 0)),
            pl.BlockSpec((k, tn), lambda i, j: (0, j)),
            pl.BlockSpec((k, tn), lambda i, j: (0, j + go)),
            pl.BlockSpec((1, tn), lambda i, j: (0, j)),
            pl.BlockSpec((1, tn), lambda i, j: (0, j + go)),
        ],
        out_specs=pl.BlockSpec((tm, tn), lambda i, j: (i, j)),
        out_shape=jax.ShapeDtypeStruct((n, n_out), jnp.float32),
        compiler_params=_params(("parallel", "parallel"), 48),
        name="inproj_glu",
    )(h, w, w, b, b)


def _inproj_linear(h, w, b, n_out, col_off, tm, tn, sigmoid, out_dtype):
    n, k = h.shape
    co = col_off // tn
    return pl.pallas_call(
        functools.partial(_linear_body, sigmoid=sigmoid),
        grid=(n // tm, n_out // tn),
        in_specs=[
            pl.BlockSpec((tm, k), lambda i, j: (i, 0)),
            pl.BlockSpec((k, tn), lambda i, j: (0, j + co)),
            pl.BlockSpec((1, tn), lambda i, j: (0, j + co)),
        ],
        out_specs=pl.BlockSpec((tm, tn), lambda i, j: (i, j)),
        out_shape=jax.ShapeDtypeStruct((n, n_out), out_dtype),
        compiler_params=_params(("parallel", "parallel"), 48),
        name="inproj_sig" if sigmoid else "inproj_lin",
    )(h, w, b)


def _seq_pos(i, nta, tpsa, tpsb):
    in_a = i < nta
    pos = jnp.where(in_a, lax.rem(i, tpsa), lax.rem(jnp.maximum(i - nta, 0), tpsb))
    tps = jnp.where(in_a, tpsa, tpsb)
    return pos, tps


def _conv_body(prev_ref, cur_ref, next_ref, w_ref, b_ref, g_ref, beta_ref, o_ref,
               ext_ref, conv_ref, *, ts, nch, ntaps, nta, tpsa, tpsb):
    i = pl.program_id(0)
    pos, tps = _seq_pos(i, nta, tpsa, tpsb)
    keep_prev = (pos > 0).astype(jnp.float32)
    keep_next = (pos < tps - 1).astype(jnp.float32)
    for c in range(nch):
        cs = slice(c * LANE, (c + 1) * LANE)
        ext_ref[c, 0:CONV_HALO, :] = prev_ref[:, cs] * keep_prev
        ext_ref[c, CONV_HALO:CONV_HALO + ts, :] = cur_ref[:, cs]
        ext_ref[c, CONV_HALO + ts:2 * CONV_HALO + ts, :] = next_ref[:, cs] * keep_next

    pad = ntaps // 2
    span = ts + 3 * SUBLANE

    def chunk(c, carry):
        wc = w_ref[c]
        acc = jnp.broadcast_to(b_ref[c], (ts, LANE))
        for r in range(SUBLANE):
            sh = ext_ref[c, pl.ds(r, span), :]
            for q in range(4):
                k = SUBLANE * q + r - (CONV_HALO - pad)
                if 0 <= k < ntaps:
                    acc = acc + wc[k:k + 1, :] * sh[SUBLANE * q:SUBLANE * q + ts, :]
        conv_ref[c] = acc
        return carry

    lax.fori_loop(0, nch, chunk, 0)

    c_tot = float(nch * LANE)
    tot = conv_ref[0]
    for c in range(1, nch):
        tot = tot + conv_ref[c]
    mu = jnp.sum(tot, axis=-1, keepdims=True) / c_tot
    d0 = conv_ref[0] - mu
    sq = d0 * d0
    for c in range(1, nch):
        dc = conv_ref[c] - mu
        sq = sq + dc * dc
    inv = lax.rsqrt(jnp.sum(sq, axis=-1, keepdims=True) / c_tot + EPS)
    for c in range(nch):
        cs = slice(c * LANE, (c + 1) * LANE)
        y = (conv_ref[c] - mu) * inv * g_ref[:, cs] + beta_ref[:, cs]
        o_ref[:, cs] = (y * jax.nn.sigmoid(y)).astype(o_ref.dtype)


def _conv_mixer(a, w3, b3, ln_g, ln_b, ts, nta, tpsa, tpsb, ntaps):
    n, c = a.shape
    nch = c // LANE
    hb = ts // CONV_HALO
    nhb = n // CONV_HALO
    return pl.pallas_call(
        functools.partial(_conv_body, ts=ts, nch=nch, ntaps=ntaps, nta=nta, tpsa=tpsa, tpsb=tpsb),
        grid=(n // ts,),
        in_specs=[
            pl.BlockSpec((CONV_HALO, c), lambda i: (jnp.maximum(i * hb - 1, 0), 0)),
            pl.BlockSpec((ts, c), lambda i: (i, 0)),
            pl.BlockSpec((CONV_HALO, c), lambda i: (jnp.minimum((i + 1) * hb, nhb - 1), 0)),
            pl.BlockSpec((nch, 32, LANE), lambda i: (0, 0, 0)),
            pl.BlockSpec((nch, 1, LANE), lambda i: (0, 0, 0)),
            pl.BlockSpec((1, c), lambda i: (0, 0)),
            pl.BlockSpec((1, c), lambda i: (0, 0)),
        ],
        out_specs=pl.BlockSpec((ts, c), lambda i: (i, 0)),
        out_shape=jax.ShapeDtypeStruct((n, c), jnp.bfloat16),
        scratch_shapes=[
            pltpu.VMEM((nch, ts + 2 * CONV_HALO, LANE), jnp.float32),
            pltpu.VMEM((nch, ts, LANE), jnp.float32),
        ],
        compiler_params=_params(("parallel",), 40),
        name="conv_mixer",
    )(a, a, a, w3, b3, ln_g, ln_b)


def _pool_body(prev_ref, cur_ref, next_ref, wmix_ref, scale_ref, o_ref, ext_ref,
               *, ts, nch, ngroups, nta, tpsa, tpsb):
    i = pl.program_id(0)
    pos, tps = _seq_pos(i, nta, tpsa, tpsb)
    keep_prev = (pos > 0).astype(jnp.float32)
    keep_next = (pos < tps - 1).astype(jnp.float32)
    for c in range(nch):
        cs = slice(c * LANE, (c + 1) * LANE)
        ext_ref[c, 0:POOL_HALO, :] = prev_ref[:, cs] * keep_prev
        ext_ref[c, POOL_HALO:POOL_HALO + ts, :] = cur_ref[:, cs]
        ext_ref[c, POOL_HALO + ts:2 * POOL_HALO + ts, :] = next_ref[:, cs] * keep_next

    t_seq = pos * ts + lax.broadcasted_iota(jnp.int32, (ts, LANE), 0)
    seq_len = tps * ts
    cpg = nch // ngroups
    for g in range(ngroups):
        window = 2 ** (g + 1)
        left = window // 2
        right = window - left - 1
        count = (jnp.minimum(t_seq + right + 1, seq_len) - jnp.maximum(t_seq - left, 0)).astype(jnp.float32)
        zs = []
        for c in range(g * cpg, (g + 1) * cpg):
            tot = ext_ref[c, pl.ds(POOL_HALO - left, ts), :]
            for j in range(1, window):
                tot = tot + ext_ref[c, pl.ds(POOL_HALO - left + j, ts), :]
            zs.append(tot / count - ext_ref[c, pl.ds(POOL_HALO, ts), :])
        z = jnp.concatenate(zs, axis=-1).astype(jnp.bfloat16)
        gs = slice(g * cpg * LANE, (g + 1) * cpg * LANE)
        zm = jnp.dot(z, wmix_ref[g], preferred_element_type=jnp.float32)
        o_ref[:, gs] = (zm * scale_ref[:, gs]).astype(o_ref.dtype)


def _pool_mixer(u, wmix, scale, ts, nta, tpsa, tpsb):
    n, c = u.shape
    nch = c // LANE
    ngroups, gd, _ = wmix.shape
    hb = ts // POOL_HALO
    nhb = n // POOL_HALO
    return pl.pallas_call(
        functools.partial(_pool_body, ts=ts, nch=nch, ngroups=ngroups, nta=nta, tpsa=tpsa, tpsb=tpsb),
        grid=(n // ts,),
        in_specs=[
            pl.BlockSpec((POOL_HALO, c), lambda i: (jnp.maximum(i * hb - 1, 0), 0)),
            pl.BlockSpec((ts, c), lambda i: (i, 0)),
            pl.BlockSpec((POOL_HALO, c), lambda i: (jnp.minimum((i + 1) * hb, nhb - 1), 0)),
            pl.BlockSpec((ngroups, gd, gd), lambda i: (0, 0, 0)),
            pl.BlockSpec((1, c), lambda i: (0, 0)),
        ],
        out_specs=pl.BlockSpec((ts, c), lambda i: (i, 0)),
        out_shape=jax.ShapeDtypeStruct((n, c), jnp.bfloat16),
        scratch_shapes=[pltpu.VMEM((nch, ts + 2 * POOL_HALO, LANE), jnp.float32)],
        compiler_params=_params(("parallel",), 40),
        name="pool_mixer",
    )(u, u, u, wmix, scale)


def _merge_body(a_ref, z_ref, wa_ref, wz_ref, sa_ref, sz_ref, o_ref):
    ya = jnp.dot(a_ref[...], wa_ref[...], preferred_element_type=jnp.float32)
    yz = jnp.dot(z_ref[...], wz_ref[...], preferred_element_type=jnp.float32)
    mixed = sa_ref[...].astype(jnp.float32) * ya + sz_ref[...].astype(jnp.float32) * yz
    o_ref[...] = mixed.astype(o_ref.dtype)


def _merge(a2, zs, wa, wz, gates, tm, tn):
    n, k = a2.shape
    d = wa.shape[1]
    zo = d // tn
    return pl.pallas_call(
        _merge_body,
        grid=(n // tm, d // tn),
        in_specs=[
            pl.BlockSpec((tm, k), lambda i, j: (i, 0)),
            pl.BlockSpec((tm, k), lambda i, j: (i, 0)),
            pl.BlockSpec((k, tn), lambda i, j: (0, j)),
            pl.BlockSpec((k, tn), lambda i, j: (0, j)),
            pl.BlockSpec((tm, tn), lambda i, j: (i, j)),
            pl.BlockSpec((tm, tn), lambda i, j: (i, j + zo)),
        ],
        out_specs=pl.BlockSpec((tm, tn), lambda i, j: (i, j)),
        out_shape=jax.ShapeDtypeStruct((n, d), jnp.bfloat16),
        compiler_params=_params(("parallel", "parallel"), 48),
        name="merge",
    )(a2, zs, wa, wz, gates, gates)


def _outproj_body(m_ref, w_ref, xa_ref, xb_ref, o_ref, *, nta):
    i = pl.program_id(0)
    x = jnp.where(i < nta, xa_ref[...], xb_ref[...])
    o_ref[...] = x + jnp.dot(m_ref[...], w_ref[...], preferred_element_type=jnp.float32)


def _outproj(mixed, w, xa, xb, tm, tn):
    n, k = mixed.shape
    d = w.shape[1]
    nta = xa.shape[0] // tm
    return pl.pallas_call(
        functools.partial(_outproj_body, nta=nta),
        grid=(n // tm, d // tn),
        in_specs=[
            pl.BlockSpec((tm, k), lambda i, j: (i, 0)),
            pl.BlockSpec((k, tn), lambda i, j: (0, j)),
            pl.BlockSpec((tm, tn), lambda i, j: (jnp.minimum(i, nta - 1), jnp.where(i < nta, j, 0))),
            pl.BlockSpec((tm, tn), lambda i, j: (jnp.maximum(i - nta, 0), jnp.where(i < nta, 0, j))),
        ],
        out_specs=pl.BlockSpec((tm, tn), lambda i, j: (i, j)),
        out_shape=jax.ShapeDtypeStruct((n, d), jnp.float32),
        compiler_params=_params(("parallel", "parallel"), 48),
        name="outproj",
    )(mixed, w, xa, xb)


def _pack_rows(y):
    dh = y.shape[-1] // 2
    hi = lax.bitcast_convert_type(y[:, :dh].astype(jnp.bfloat16).astype(jnp.float32), jnp.uint32)
    lo = lax.bitcast_convert_type(y[:, dh:].astype(jnp.bfloat16).astype(jnp.float32), jnp.uint32)
    return hi | (lo >> 16)


def _store_records(rec_ref, packed, rows, nsl):
    for s in range(nsl):
        rec_ref[pl.ds(s, rows, stride=nsl), :] = packed[:, s * LANE:(s + 1) * LANE]


def _load_record_halves(ref, base, rows, nsl, slot=None):
    his, los = [], []
    for s in range(nsl):
        if slot is None:
            p = ref[pl.ds(base + s, rows, stride=nsl), :]
        else:
            p = ref[slot, pl.ds(base + s, rows, stride=nsl), :]
        his.append(lax.bitcast_convert_type(p & jnp.uint32(0xFFFF0000), jnp.float32))
        los.append(lax.bitcast_convert_type(p << 16, jnp.float32))
    return jnp.concatenate(his, axis=-1), jnp.concatenate(los, axis=-1)


def _route_body(x_ref, g_ref, whi_ref, wlo_ref, rec_ref, meta_ref, comb_ref, cnt_ref, carry_ref,
                *, tm, nsl, ngroups, epg):
    i = pl.program_id(0)

    @pl.when(i == 0)
    def _():
        carry_ref[...] = jnp.zeros_like(carry_ref)

    x = x_ref[...]
    ms = jnp.mean(x * x, axis=-1, keepdims=True)
    hn = x * lax.rsqrt(ms + EPS) * g_ref[...]
    _store_records(rec_ref, _pack_rows(hn), tm, nsl)

    h_hi = hn.astype(jnp.bfloat16)
    h_lo = (hn - h_hi.astype(jnp.float32)).astype(jnp.bfloat16)
    logits = (jnp.dot(h_hi, whi_ref[...], preferred_element_type=jnp.float32)
              + jnp.dot(h_hi, wlo_ref[...], preferred_element_type=jnp.float32)
              + jnp.dot(h_lo, whi_ref[...], preferred_element_type=jnp.float32))

    lane = lax.broadcasted_iota(jnp.int32, (tm, LANE), 1)
    lane_f = lane.astype(jnp.float32)
    big = float(LANE)
    is_g = lane < ngroups
    gl = jnp.where(is_g, logits, NEG)
    gmax = jnp.max(gl, axis=-1, keepdims=True)
    gsel = jnp.min(jnp.where(is_g & (gl == gmax), lane_f, big), axis=-1, keepdims=True)
    gden = jnp.sum(jnp.where(is_g, jnp.exp(gl - gmax), 0.0), axis=-1, keepdims=True)
    gweight = 1.0 / gden

    e_lo = ngroups + gsel * epg
    is_e = (lane_f >= e_lo) & (lane_f < e_lo + epg)
    el = jnp.where(is_e, logits, NEG)
    emax = jnp.max(el, axis=-1, keepdims=True)
    ep = jnp.where(is_e, jnp.exp(el - emax), 0.0)
    eprob = ep / jnp.sum(ep, axis=-1, keepdims=True)
    p1 = jnp.where(is_e, eprob, -1.0)
    v1 = jnp.max(p1, axis=-1, keepdims=True)
    i1 = jnp.min(jnp.where(p1 == v1, lane_f, big), axis=-1, keepdims=True)
    p2 = jnp.where(lane_f == i1, -1.0, p1)
    v2 = jnp.max(p2, axis=-1, keepdims=True)
    i2 = jnp.min(jnp.where(p2 == v2, lane_f, big), axis=-1, keepdims=True)
    denom = v1 + v2
    c1 = gweight * v1 / denom
    c2 = gweight * v2 / denom
    eid1 = i1 - ngroups
    eid2 = i2 - ngroups

    onehot = jnp.where((lane_f == eid1) | (lane_f == eid2), 1.0, 0.0)
    row = lax.broadcasted_iota(jnp.int32, (tm, tm), 0)
    col = lax.broadcasted_iota(jnp.int32, (tm, tm), 1)
    tri = jnp.where(col < row, 1.0, 0.0).astype(jnp.bfloat16)
    before = jnp.dot(tri, onehot.astype(jnp.bfloat16), preferred_element_type=jnp.float32) + carry_ref[0:1, :]
    r1 = jnp.sum(jnp.where(lane_f == eid1, before, 0.0), axis=-1, keepdims=True)
    r2 = jnp.sum(jnp.where(lane_f == eid2, before, 0.0), axis=-1, keepdims=True)
    carry_ref[0:1, :] = carry_ref[0:1, :] + jnp.sum(onehot, axis=0, keepdims=True)

    meta = jnp.where(lane == 0, eid1, jnp.where(lane == 1, eid2, jnp.where(lane == 2, r1, jnp.where(lane == 3, r2, 0.0))))
    meta_ref[...] = meta.astype(jnp.int32)
    comb_ref[...] = jnp.where(lane == 0, c1, jnp.where(lane == 1, c2, 0.0))
    cnt_ref[...] = jnp.broadcast_to(carry_ref[0:1, :], cnt_ref.shape).astype(jnp.int32)


def _route(x1, g, whi, wlo, tm, ngroups, epg):
    n, d = x1.shape
    nsl = d // 2 // LANE
    return pl.pallas_call(
        functools.partial(_route_body, tm=tm, nsl=nsl, ngroups=ngroups, epg=epg),
        grid=(n // tm,),
        in_specs=[
            pl.BlockSpec((tm, d), lambda i: (i, 0)),
            pl.BlockSpec((1, d), lambda i: (0, 0)),
            pl.BlockSpec((d, LANE), lambda i: (0, 0)),
            pl.BlockSpec((d, LANE), lambda i: (0, 0)),
        ],
        out_specs=[
            pl.BlockSpec((tm * nsl, LANE), lambda i: (i, 0)),
            pl.BlockSpec((tm, LANE), lambda i: (i, 0)),
            pl.BlockSpec((tm, LANE), lambda i: (i, 0)),
            pl.BlockSpec((SUBLANE, LANE), lambda i: (0, 0)),
        ],
        out_shape=[
            jax.ShapeDtypeStruct((n * nsl, LANE), jnp.uint32),
            jax.ShapeDtypeStruct((n, LANE), jnp.int32),
            jax.ShapeDtypeStruct((n, LANE), jnp.float32),
            jax.ShapeDtypeStruct((SUBLANE, LANE), jnp.int32),
        ],
        scratch_shapes=[pltpu.VMEM((SUBLANE, LANE), jnp.float32)],
        compiler_params=_params(("arbitrary",), 40),
        name="route",
    )(x1, g, whi, wlo)


def _record_copy(src_hbm, buf, sem, tok, r, slot, nsl):
    return pltpu.make_async_copy(src_hbm.at[pl.ds(tok * nsl, nsl), :],
                                 buf.at[slot, pl.ds(r * nsl, nsl), :], sem.at[slot])


def _gather_start(idx_ref, src_hbm, buf, sem, slot, count, nsl):
    def body(r, carry):
        _record_copy(src_hbm, buf, sem, idx_ref[0, 0, r], r, slot, nsl).start()
        return carry
    lax.fori_loop(0, count, body, 0)


def _gather_wait(idx_ref, src_hbm, buf, sem, slot, count, nsl):
    def body(r, carry):
        _record_copy(src_hbm, buf, sem, idx_ref[0, 0, r], r, slot, nsl).wait()
        return carry
    lax.fori_loop(0, count, body, 0)


def _up_body(be_ref, nact_ref, tok_ref, tok_next_ref, rec_hbm, w_ref, o_ref, buf, sem, *, rb, nsl, f):
    b = pl.program_id(0)
    nact = nact_ref[0]
    slot = lax.rem(b, 2)

    @pl.when(b == 0)
    def _():
        _gather_start(tok_ref, rec_hbm, buf, sem, 0, rb, nsl)

    @pl.when(b + 1 < nact)
    def _():
        _gather_start(tok_next_ref, rec_hbm, buf, sem, 1 - slot, rb, nsl)

    @pl.when(b < nact)
    def _():
        _gather_wait(tok_ref, rec_hbm, buf, sem, slot, rb, nsl)
        x_hi, x_lo = _load_record_halves(buf, 0, rb, nsl, slot=slot)
        dh = nsl * LANE
        gu = (jnp.dot(x_hi.astype(jnp.bfloat16), w_ref[0, :dh, :], preferred_element_type=jnp.float32)
              + jnp.dot(x_lo.astype(jnp.bfloat16), w_ref[0, dh:, :], preferred_element_type=jnp.float32))
        gate = gu[:, :f]
        o_ref[...] = (gate * jax.nn.sigmoid(gate) * gu[:, f:]).astype(o_ref.dtype)

    @pl.when(b >= nact)
    def _():
        o_ref[...] = jnp.zeros_like(o_ref)


def _expert_up(blk_expert, nact, row_tok, rec, w_gu, rb):
    ne, d, f2 = w_gu.shape
    f = f2 // 2
    nb = row_tok.shape[0]
    nsl = d // 2 // LANE
    grid_spec = pltpu.PrefetchScalarGridSpec(
        num_scalar_prefetch=2,
        grid=(nb,),
        in_specs=[
            pl.BlockSpec((1, 1, rb), lambda b, be, na: (b, 0, 0), memory_space=pltpu.SMEM),
            pl.BlockSpec((1, 1, rb), lambda b, be, na: (jnp.minimum(b + 1, nb - 1), 0, 0), memory_space=pltpu.SMEM),
            pl.BlockSpec(memory_space=pl.ANY),
            pl.BlockSpec((1, d, f2), lambda b, be, na: (be[b], 0, 0)),
        ],
        out_specs=pl.BlockSpec((rb, f), lambda b, be, na: (b, 0)),
        scratch_shapes=[
            pltpu.VMEM((2, rb * nsl, LANE), jnp.uint32),
            pltpu.SemaphoreType.DMA((2,)),
        ],
    )
    return pl.pallas_call(
        functools.partial(_up_body, rb=rb, nsl=nsl, f=f),
        grid_spec=grid_spec,
        out_shape=jax.ShapeDtypeStruct((nb * rb, f), jnp.bfloat16),
        compiler_params=_params(("arbitrary",), 52),
        name="expert_up",
    )(blk_expert, nact, row_tok, row_tok, rec, w_gu)


def _down_body(be_ref, nact_ref, h_ref, w_ref, rec_ref, *, rb, nsl):
    b = pl.program_id(0)

    @pl.when(b < nact_ref[0])
    def _():
        y = jnp.dot(h_ref[...], w_ref[0], preferred_element_type=jnp.float32)
        _store_records(rec_ref, _pack_rows(y), rb, nsl)

    @pl.when(b >= nact_ref[0])
    def _():
        rec_ref[...] = jnp.zeros_like(rec_ref)


def _expert_down(blk_expert, nact, hmid, w_down, rb):
    ne, f, d = w_down.shape
    nb = hmid.shape[0] // rb
    nsl = d // 2 // LANE
    grid_spec = pltpu.PrefetchScalarGridSpec(
        num_scalar_prefetch=2,
        grid=(nb,),
        in_specs=[
            pl.BlockSpec((rb, f), lambda b, be, na: (b, 0)),
            pl.BlockSpec((1, f, d), lambda b, be, na: (be[b], 0, 0)),
        ],
        out_specs=pl.BlockSpec((rb * nsl, LANE), lambda b, be, na: (b, 0)),
    )
    return pl.pallas_call(
        functools.partial(_down_body, rb=rb, nsl=nsl),
        grid_spec=grid_spec,
        out_shape=jax.ShapeDtypeStruct((nb * rb * nsl, LANE), jnp.uint32),
        compiler_params=_params(("arbitrary",), 40),
        name="expert_down",
    )(blk_expert, nact, hmid, w_down)


def _combine_body(dst_ref, dst_next_ref, rec_hbm, x_ref, comb_ref, g_ref, x2_ref, hp_ref, buf, sem,
                  *, tm, nsl):
    i = pl.program_id(0)
    nt = pl.num_programs(0)
    slot = lax.rem(i, 2)
    cnt = TOP_K * tm

    @pl.when(i == 0)
    def _():
        _gather_start(dst_ref, rec_hbm, buf, sem, 0, cnt, nsl)

    @pl.when(i + 1 < nt)
    def _():
        _gather_start(dst_next_ref, rec_hbm, buf, sem, 1 - slot, cnt, nsl)

    _gather_wait(dst_ref, rec_hbm, buf, sem, slot, cnt, nsl)
    dh = nsl * LANE
    c = comb_ref[...]
    c1 = c[:, 0:1]
    c2 = c[:, 1:2]
    hi1, lo1 = _load_record_halves(buf, 0, tm, nsl, slot=slot)
    hi2, lo2 = _load_record_halves(buf, tm * nsl, tm, nsl, slot=slot)
    xa = x_ref[:, :dh] + (c1 * hi1 + c2 * hi2)
    xb = x_ref[:, dh:] + (c1 * lo1 + c2 * lo2)
    x2_ref[:, :dh] = xa
    x2_ref[:, dh:] = xb
    ms = (jnp.sum(xa * xa, axis=-1, keepdims=True) + jnp.sum(xb * xb, axis=-1, keepdims=True)) / float(2 * dh)
    inv = lax.rsqrt(ms + EPS)
    hp_ref[:, :dh] = (xa * inv * g_ref[:, :dh]).astype(hp_ref.dtype)
    hp_ref[:, dh:] = (xb * inv * g_ref[:, dh:]).astype(hp_ref.dtype)


def _combine(dest_tiles, rec, x1, comb, g, tm):
    n, d = x1.shape
    nt = n // tm
    nsl = d // 2 // LANE
    return pl.pallas_call(
        functools.partial(_combine_body, tm=tm, nsl=nsl),
        grid=(nt,),
        in_specs=[
            pl.BlockSpec((1, 1, TOP_K * tm), lambda i: (i, 0, 0), memory_space=pltpu.SMEM),
            pl.BlockSpec((1, 1, TOP_K * tm), lambda i: (jnp.minimum(i + 1, nt - 1), 0, 0), memory_space=pltpu.SMEM),
            pl.BlockSpec(memory_space=pl.ANY),
            pl.BlockSpec((tm, d), lambda i: (i, 0)),
            pl.BlockSpec((tm, LANE), lambda i: (i, 0)),
            pl.BlockSpec((1, d), lambda i: (0, 0)),
        ],
        out_specs=[
            pl.BlockSpec((tm, d), lambda i: (i, 0)),
            pl.BlockSpec((tm, d), lambda i: (i, 0)),
        ],
        out_shape=[
            jax.ShapeDtypeStruct((n, d), jnp.float32),
            jax.ShapeDtypeStruct((n, d), jnp.bfloat16),
        ],
        scratch_shapes=[
            pltpu.VMEM((2, TOP_K * tm * nsl, LANE), jnp.uint32),
            pltpu.SemaphoreType.DMA((2,)),
        ],
        compiler_params=_params(("arbitrary",), 48),
        name="combine",
    )(dest_tiles, dest_tiles, rec, x1, comb, g)


def _ple_body(hp_ref, wg_ref, p_ref, wp_ref, x_ref, o_ref):
    gate = jax.nn.sigmoid(jnp.dot(hp_ref[...], wg_ref[...], preferred_element_type=jnp.float32))
    e = jnp.dot(p_ref[...].astype(jnp.bfloat16), wp_ref[...], preferred_element_type=jnp.float32)
    o_ref[...] = x_ref[...] + e * gate


def _ple(hp, wg, p, wp, x2, tm, tn):
    n, d = x2.shape
    dp = p.shape[1]
    return pl.pallas_call(
        _ple_body,
        grid=(n // tm, d // tn),
        in_specs=[
            pl.BlockSpec((tm, d), lambda i, j: (i, 0)),
            pl.BlockSpec((d, tn), lambda i, j: (0, j)),
            pl.BlockSpec((tm, dp), lambda i, j: (i, 0)),
            pl.BlockSpec((dp, tn), lambda i, j: (0, j)),
            pl.BlockSpec((tm, tn), lambda i, j: (i, j)),
        ],
        out_specs=pl.BlockSpec((tm, tn), lambda i, j: (i, j)),
        out_shape=jax.ShapeDtypeStruct((n, d), jnp.float32),
        compiler_params=_params(("parallel", "parallel"), 48),
        name="ple",
    )(hp, wg, p, wp, x2)


def _final_body(x_ref, g_ref, o_ref):
    x = x_ref[...]
    ms = jnp.mean(x * x, axis=-1, keepdims=True)
    o_ref[...] = x * lax.rsqrt(ms + EPS) * g_ref[...]


def _final_norm(x, g, row_off, rows, tm):
    d = x.shape[1]
    off = row_off // tm
    return pl.pallas_call(
        _final_body,
        grid=(rows // tm,),
        in_specs=[
            pl.BlockSpec((tm, d), lambda i: (i + off, 0)),
            pl.BlockSpec((1, d), lambda i: (0, 0)),
        ],
        out_specs=pl.BlockSpec((tm, d), lambda i: (i, 0)),
        out_shape=jax.ShapeDtypeStruct((rows, d), jnp.float32),
        compiler_params=_params(("parallel",), 40),
        name="final_norm",
    )(x, g)


def kernel(x_prompt, x_sample, p_prompt, p_sample, w_in, b_in, w_dw, b_dw, ln_g, ln_b, w_pool_mix, pool_scale, w_conv_out, w_pool_out, w_out, g_mix, g_ffn, w_group_router, w_expert_router, w_gate_up, w_down, g_ple, w_ple_proj, w_ple_gate, g_final):
    ba, sa, d = x_prompt.shape
    bb, sb, _ = x_sample.shape
    na, nb_tok = ba * sa, bb * sb
    n = na + nb_tok
    depth = w_in.shape[0]
    ntaps, dc = w_dw.shape[1], w_dw.shape[2]
    dpool = pool_scale.shape[1]
    ngroups = w_group_router.shape[2]
    nexp = w_expert_router.shape[2]
    epg = nexp // ngroups
    assert ngroups + nexp <= LANE and ntaps // 2 < CONV_HALO and ntaps <= 32
    assert w_pool_mix.shape[1] == len((2, 4, 8, 16)) and d % (2 * LANE) == 0

    f32, bf16 = jnp.float32, jnp.bfloat16
    xa = x_prompt.reshape(na, d)
    xb = x_sample.reshape(nb_tok, d)

    seq_gcd = int(np.gcd(sa, sb))
    ts = _tile(seq_gcd, 256, CONV_HALO)
    tpsa, tpsb = sa // ts, sb // ts
    nta = na // ts
    t_row = _tile(int(np.gcd(na, nb_tok)), 256)
    t_mm = _tile(int(np.gcd(na, nb_tok)), 1024, 16)
    tn = _tile(d, 512, LANE)
    tn_c = _tile(dc, 512, LANE)
    rb = ROW_BLOCK
    n_blocks = (n * TOP_K) // rb + nexp

    for li in range(depth):
        w_in_b = w_in[li].astype(bf16)
        b_in_r = b_in[li].reshape(1, -1)
        h = _prenorm(xa, xb, g_mix[li].reshape(1, d), t_row)
        a = _inproj_glu(h, w_in_b, b_in_r, dc, dc, t_mm, tn_c)
        u = _inproj_linear(h, w_in_b, b_in_r, dpool, 2 * dc, t_mm, tn_c, False, f32)
        gates = _inproj_linear(h, w_in_b, b_in_r, 2 * d, 2 * dc + dpool, t_mm, tn, True, bf16)

        nch = dc // LANE
        w3 = jnp.pad(w_dw[li], ((0, 32 - ntaps), (0, 0))).reshape(32, nch, LANE).transpose(1, 0, 2)
        b3 = b_dw[li].reshape(nch, 1, LANE)
        a2 = _conv_mixer(a, w3, b3, ln_g[li].reshape(1, dc), ln_b[li].reshape(1, dc),
                         ts, nta, tpsa, tpsb, ntaps)
        zs = _pool_mixer(u, w_pool_mix[li].astype(bf16), pool_scale[li].reshape(1, dpool),
                         ts, nta, tpsa, tpsb)
        mixed = _merge(a2, zs, w_conv_out[li].astype(bf16), w_pool_out[li].astype(bf16), gates, t_mm, tn)
        x1 = _outproj(mixed, w_out[li].astype(bf16), xa, xb, t_mm, tn)

        w_r = jnp.concatenate([w_group_router[li], w_expert_router[li]], axis=1)
        w_r = jnp.pad(w_r, ((0, 0), (0, LANE - w_r.shape[1])))
        w_r_hi = w_r.astype(bf16)
        w_r_lo = (w_r - w_r_hi.astype(f32)).astype(bf16)
        rec, meta, comb, counts = _route(x1, g_ffn[li].reshape(1, d), w_r_hi, w_r_lo, t_row, ngroups, epg)

        eid = meta[:, 0:TOP_K]
        rank = meta[:, TOP_K:2 * TOP_K]
        cnt = counts[0, :nexp]
        padded = (cnt + rb - 1) // rb * rb
        pend = jnp.cumsum(padded)
        pstart = pend - padded
        dest = pstart[eid] + rank
        tok = jnp.broadcast_to(jnp.arange(n, dtype=jnp.int32)[:, None], (n, TOP_K))
        row_tok = jnp.zeros((n_blocks * rb,), jnp.int32).at[dest.reshape(-1)].set(tok.reshape(-1))
        blk_start = jnp.arange(n_blocks, dtype=jnp.int32) * rb
        blk_expert = jnp.minimum(jnp.searchsorted(pend, blk_start, side='right'), nexp - 1).astype(jnp.int32)
        nact = (pend[-1] // rb).astype(jnp.int32).reshape(1)

        hmid = _expert_up(blk_expert, nact, row_tok.reshape(n_blocks, 1, rb), rec, w_gate_up[li].astype(bf16), rb)
        yrec = _expert_down(blk_expert, nact, hmid, w_down[li].astype(bf16), rb)

        nt_c = n // t_row
        dest_tiles = dest.reshape(nt_c, t_row, TOP_K).transpose(0, 2, 1).reshape(nt_c, 1, TOP_K * t_row)
        x2, hp = _combine(dest_tiles, yrec, x1, comb, g_ple[li].reshape(1, d), t_row)

        p_all = jnp.concatenate([p_prompt[li].reshape(na, -1), p_sample[li].reshape(nb_tok, -1)], axis=0)
        x3 = _ple(hp, w_ple_gate[li].astype(bf16), p_all, w_ple_proj[li].astype(bf16), x2, t_mm, tn)
        if li + 1 < depth:
            xa, xb = x3[:na], x3[na:]

    gf = g_final.reshape(1, d)
    y_a = _final_norm(x3, gf, 0, na, t_row).reshape(ba, sa, d)
    y_b = _final_norm(x3, gf, na, nb_tok, t_row).reshape(bb, sb, d)
    return (y_a, y_b)
```
